```python
import jax
import jax.numpy as jnp
from jax import lax
import numpy as np

D_MODEL = 2048
BATCH = 2
SEQ = 4096
DEPTH = 2

GRID_W = 64

POOL_WINDOWS = (2, 4, 8, 16)
POOL_GROUPS = 4
POOL_GROUP_W = D_MODEL // 16
POOL_W = POOL_GROUPS * POOL_GROUP_W

ATT_HEAD_DIM = 128
ATT_HEADS = (D_MODEL - POOL_W) // ATT_HEAD_DIM
ATT_KV_HEADS = 4
ATT_GROUP = ATT_HEADS // ATT_KV_HEADS
Q_W = ATT_HEADS * ATT_HEAD_DIM
KV_W = ATT_KV_HEADS * ATT_HEAD_DIM
Q_BLOCK = 128
ROPE_THETA = 10000.0
AB_IN_W = POOL_W + Q_W + 2 * KV_W
AB_OUT_W = POOL_W + Q_W

MLSTM_HEADS = 8
MLSTM_HEAD_DIM = D_MODEL // MLSTM_HEADS
MLSTM_W = MLSTM_HEADS * MLSTM_HEAD_DIM
MLSTM_CHUNK = 64
C_IN_W = 4 * MLSTM_W + 4 * MLSTM_HEADS
FORGET_BIAS = 3.0

D_FF = 5632
CONV_WIDTH = 3

NORM_EPS = 1e-6

kernel_name = "hybrid_pool_gqa_mlstm_convffn_encoder"


def rms_norm(x, g):
    xf = x.astype(jnp.float32)
    y = xf * lax.rsqrt(jnp.mean(xf * xf, axis=-1, keepdims=True) + NORM_EPS)
    return (y * g.astype(jnp.float32)).astype(x.dtype)


def axial_rope(seq_len):
    rows = seq_len // GRID_W
    row = jnp.repeat(jnp.arange(rows), GRID_W).astype(jnp.float32)
    col = jnp.tile(jnp.arange(GRID_W), rows).astype(jnp.float32)
    half = ATT_HEAD_DIM // 2
    inv = 1.0 / (ROPE_THETA ** (jnp.arange(0, half, 2, dtype=jnp.float32) / half))
    a_r = row[:, None] * inv[None, :]
    a_c = col[:, None] * inv[None, :]
    ang = jnp.concatenate([a_r, a_r, a_c, a_c], axis=-1)
    return jnp.cos(ang)[:, None, :], jnp.sin(ang)[:, None, :]


def apply_rope(x, cos, sin):
    xf = x.astype(jnp.float32)
    half = ATT_HEAD_DIM // 2
    quarter = ATT_HEAD_DIM // 4

    def rot(z):
        return jnp.concatenate([-z[..., quarter:], z[..., :quarter]], axis=-1)

    rotated = jnp.concatenate([rot(xf[..., :half]), rot(xf[..., half:])], axis=-1)
    return (xf * cos + rotated * sin).astype(x.dtype)


def multiscale_pool(u, pool_w, pool_scale):
    b, s, _ = u.shape
    uf = u.astype(jnp.float32).reshape(b, s, POOL_GROUPS, POOL_GROUP_W)
    cs = jnp.concatenate(
        [jnp.zeros((b, 1, POOL_GROUPS, POOL_GROUP_W), jnp.float32), jnp.cumsum(uf, axis=1)], axis=1
    )
    t = jnp.arange(s)
    means = []
    for gi, w in enumerate(POOL_WINDOWS):
        lo = jnp.clip(t - w // 2, 0, s)
        hi = jnp.clip(t + w // 2, 0, s)
        csg = cs[:, :, gi]
        window_sum = jnp.take(csg, hi, axis=1) - jnp.take(csg, lo, axis=1)
        means.append(window_sum / (hi - lo).astype(jnp.float32)[None, :, None])
    pooled = jnp.stack(means, axis=2) - uf
    y = jnp.einsum("bsgc,gcd->bsgd", pooled, pool_w.astype(jnp.float32))
    y = y.reshape(b, s, POOL_W) * pool_scale.astype(jnp.float32)
    return y.astype(u.dtype)


def blocked_gqa(q, k, v):
    b, s, _, _ = q.shape
    nb = s // Q_BLOCK
    scale = ATT_HEAD_DIM ** -0.5
    qb = q.reshape(b, nb, Q_BLOCK, ATT_KV_HEADS, ATT_GROUP, ATT_HEAD_DIM).transpose(1, 0, 2, 3, 4, 5)

    def one_block(q_blk):
        scores = jnp.einsum("bqkgd,bskd->bkgqs", q_blk, k).astype(jnp.float32) * scale
        probs = jax.nn.softmax(scores, axis=-1).astype(v.dtype)
        return jnp.einsum("bkgqs,bskd->bqkgd", probs, v)

    out = lax.map(one_block, qb)
    return out.transpose(1, 0, 2, 3, 4, 5).reshape(b, s, Q_W)


def mlstm_one_direction(q, k, v, i_pre, f_pre):
    b, h, s, d = q.shape
    L = MLSTM_CHUNK
    nc = s // L

    def chunks(z):
        return jnp.moveaxis(z.reshape((b, h, nc, L) + z.shape[3:]), 2, 0)

    logf = jax.nn.log_sigmoid(f_pre)
    mask = jnp.tril(jnp.ones((L, L), dtype=bool))

    def step(carry, inp):
        c_st, n_st, m_st = carry
        qc, kc, vc, ic, lfc = inp
        bcum = jnp.cumsum(lfc, axis=-1)
        dmat = bcum[..., :, None] - bcum[..., None, :] + ic[..., None, :]
        dmat = jnp.where(mask, dmat, -jnp.inf)
        m_inter = bcum + m_st[..., None]
        m_t = jnp.maximum(m_inter, jnp.max(dmat, axis=-1))
        w = jnp.exp(dmat - m_t[..., None])
        a_inter = jnp.exp(m_inter - m_t)
        scores = jnp.einsum("bhtd,bhsd->bhts", qc, kc) * w
        num = jnp.einsum("bhts,bhse->bhte", scores, vc) + a_inter[..., None] * jnp.einsum(
            "bhtd,bhde->bhte", qc, c_st
        )
        den = jnp.sum(scores, axis=-1) + a_inter * jnp.einsum("bhtd,bhd->bht", qc, n_st)
        h_out = num / jnp.maximum(jnp.abs(den), jnp.exp(-m_t))[..., None]
        b_last = bcum[..., -1]
        g_s = b_last[..., None] - bcum + ic
        m_next = jnp.maximum(b_last + m_st, jnp.max(g_s, axis=-1))
        decay = jnp.exp(b_last + m_st - m_next)
        ws = jnp.exp(g_s - m_next[..., None])
        c_new = decay[..., None, None] * c_st + jnp.einsum("bhs,bhsd,bhse->bhde", ws, kc, vc)
        n_new = decay[..., None] * n_st + jnp.einsum("bhs,bhsd->bhd", ws, kc)
        return (c_new, n_new, m_next), h_out

    init = (
        jnp.zeros((b, h, d, d), jnp.float32),
        jnp.zeros((b, h, d), jnp.float32),
        jnp.zeros((b, h), jnp.float32),
    )
    _, hs = lax.scan(step, init, (chunks(q), chunks(k), chunks(v), chunks(i_pre), chunks(logf)))
    return jnp.moveaxis(hs, 0, 2).reshape(b, h, s, d)


def mlstm_mixer(h, w_in, b_gate, h_norm, w_out):
    b, s, _ = h.shape
    proj = h @ w_in

    def heads(z):
        return z.reshape(b, s, MLSTM_HEADS, MLSTM_HEAD_DIM).transpose(0, 2, 1, 3).astype(jnp.float32)

    q = heads(proj[..., :MLSTM_W]) * (MLSTM_HEAD_DIM ** -0.5)
    k = heads(proj[..., MLSTM_W:2 * MLSTM_W])
    v = heads(proj[..., 2 * MLSTM_W:3 * MLSTM_W])
    o_gate = jax.nn.sigmoid(proj[..., 3 * MLSTM_W:4 * MLSTM_W].astype(jnp.float32))
    gates = proj[..., 4 * MLSTM_W:].astype(jnp.float32) + b_gate.astype(jnp.float32)
    gates = gates.reshape(b, s, 4, MLSTM_HEADS).transpose(2, 0, 3, 1)
    h_fw = mlstm_one_direction(q, k, v, gates[0], gates[1])

    def flip(z):
        return jnp.flip(z, axis=2)

    h_bw = flip(mlstm_one_direction(flip(q), flip(k), flip(v), flip(gates[2]), flip(gates[3])))
    hsum = (h_fw + h_bw).transpose(0, 2, 1, 3)
    hn = hsum * lax.rsqrt(jnp.mean(hsum * hsum, axis=-1, keepdims=True) + NORM_EPS)
    hn = hn * h_norm.astype(jnp.float32).reshape(MLSTM_HEADS, MLSTM_HEAD_DIM)
    y = (hn.reshape(b, s, MLSTM_W) * o_gate).astype(h.dtype)
    return y @ w_out


def conv_ffn(x, w_up, conv_w, conv_b, w_down):
    u = x @ w_up
    up = jnp.pad(u, ((0, 0), (1, 1), (0, 0)))
    c = conv_w.astype(u.dtype)
    u = c[0] * up[:, :-2] + c[1] * up[:, 1:-1] + c[2] * up[:, 2:] + conv_b.astype(u.dtype)
    gate, val = jnp.split(u, 2, axis=-1)
    return (jax.nn.silu(gate) * val) @ w_down


def setup_inputs(seed: int = 0) -> dict:
    key = jax.random.key(seed)
    ks = jax.random.split(key, 20)
    n_even = (DEPTH + 1) // 2
    n_odd = DEPTH // 2
    f32 = jnp.float32

    def dense(k, shape, fan_in):
        return jax.random.normal(k, shape, f32) * (fan_in ** -0.5)

    def gain(k, shape):
        return 1.0 + 0.02 * jax.random.normal(k, shape, f32)

    x = jax.random.normal(ks[0], (BATCH, SEQ, D_MODEL), f32)
    norm_mix = gain(ks[1], (DEPTH, D_MODEL))
    norm_ffn = gain(ks[2], (DEPTH, D_MODEL))
    w_in_ab = dense(ks[3], (n_even, D_MODEL, AB_IN_W), D_MODEL)
    pool_w = dense(ks[4], (n_even, POOL_GROUPS, POOL_GROUP_W, POOL_GROUP_W), POOL_GROUP_W)
    pool_scale = gain(ks[5], (n_even, POOL_W))
    q_norm = gain(ks[6], (n_even, ATT_HEAD_DIM))
    k_norm = gain(ks[7], (n_even, ATT_HEAD_DIM))
    w_out_ab = dense(ks[8], (n_even, AB_OUT_W, D_MODEL), AB_OUT_W)
    w_in_c = dense(ks[9], (n_odd, D_MODEL, C_IN_W), D_MODEL)
    gate_base = jnp.array([0.0, FORGET_BIAS, 0.0, FORGET_BIAS], f32)[None, :, None]
    b_gate_c = (gate_base + 0.1 * jax.random.normal(ks[10], (n_odd, 4, MLSTM_HEADS), f32)).reshape(
        n_odd, 4 * MLSTM_HEADS
    )
    h_norm_c = gain(ks[11], (n_odd, MLSTM_W))
    w_out_c = dense(ks[12], (n_odd, MLSTM_W, D_MODEL), MLSTM_W)
    w_up = dense(ks[13], (DEPTH, D_MODEL, 2 * D_FF), D_MODEL)
    conv_w = dense(ks[14], (DEPTH, CONV_WIDTH, 2 * D_FF), CONV_WIDTH)
    conv_b = 0.02 * jax.random.normal(ks[15], (DEPTH, 2 * D_FF), f32)
    w_down = dense(ks[16], (DEPTH, D_FF, D_MODEL), D_FF)
    return {
        "x": x,
        "norm_mix": norm_mix,
        "norm_ffn": norm_ffn,
        "w_in_ab": w_in_ab,
        "pool_w": pool_w,
        "pool_scale": pool_scale,
        "q_norm": q_norm,
        "k_norm": k_norm,
        "w_out_ab": w_out_ab,
        "w_in_c": w_in_c,
        "b_gate_c": b_gate_c,
        "h_norm_c": h_norm_c,
        "w_out_c": w_out_c,
        "w_up": w_up,
        "conv_w": conv_w,
        "conv_b": conv_b,
        "w_down": w_down,
    }


def reference(x, norm_mix, norm_ffn, w_in_ab, pool_w, pool_scale, q_norm, k_norm, w_out_ab,
              w_in_c, b_gate_c, h_norm_c, w_out_c, w_up, conv_w, conv_b, w_down):
    b, s, _ = x.shape
    cos, sin = axial_rope(s)
    for layer in range(DEPTH):
        h = rms_norm(x, norm_mix[layer])
        if layer % 2 == 0:
            e = layer // 2
            proj = h @ w_in_ab[e]
            u = proj[..., :POOL_W]
            q = proj[..., POOL_W:POOL_W + Q_W].reshape(b, s, ATT_HEADS, ATT_HEAD_DIM)
            k = proj[..., POOL_W + Q_W:POOL_W + Q_W + KV_W].reshape(b, s, ATT_KV_HEADS, ATT_HEAD_DIM)
            v = proj[..., POOL_W + Q_W + KV_W:].reshape(b, s, ATT_KV_HEADS, ATT_HEAD_DIM)
            q = apply_rope(rms_norm(q, q_norm[e]), cos, sin)
            k = apply_rope(rms_norm(k, k_norm[e]), cos, sin)
            pool_out = multiscale_pool(u, pool_w[e], pool_scale[e])
            att_out = blocked_gqa(q, k, v)
            mixed = jnp.concatenate([pool_out, att_out], axis=-1) @ w_out_ab[e]
        else:
            o = layer // 2
            mixed = mlstm_mixer(h, w_in_c[o], b_gate_c[o], h_norm_c[o], w_out_c[o])
        x = x + mixed
        x = x + conv_ffn(rms_norm(x, norm_ffn[layer]), w_up[layer], conv_w[layer], conv_b[layer], w_down[layer])
    return x
```

```python
import functools
import math

import numpy as np
import jax
import jax.numpy as jnp
from jax import lax
from jax.experimental import pallas as pl
from jax.experimental.pallas import tpu as pltpu

F32 = jnp.float32
BF16 = jnp.bfloat16

NORM_EPS = 1e-6
GRID_W = 64
ROPE_THETA = 10000.0
POOL_WINDOWS = (2, 4, 8, 16)
ATT_HEAD_DIM = 128
ATT_KV_HEADS = 4
MLSTM_HEADS = 8
MLSTM_CHUNK = 256
LANES = 128
BF16_ROWS = 16
LOG2E = math.log2(math.e)
VMEM_LIMIT = 56 * 1024 * 1024

_NT = (((1,), (1,)), ((), ()))


def _params(*sem):
    return pltpu.CompilerParams(dimension_semantics=sem, vmem_limit_bytes=VMEM_LIMIT)


def _rms(x, g):
    ms = jnp.mean(x * x, axis=-1, keepdims=True)
    return x * lax.rsqrt(ms + NORM_EPS) * g


def _in_proj_ab_kernel(x_ref, g_ref, w_ref, qg_ref, kg_ref, cos_ref, sina_ref, sinb_ref,
                       proj_ref, u_ref, xn_ref, *, n_q_tiles, q_scale):
    j = pl.program_id(1)

    @pl.when(j == 0)
    def _():
        xn_ref[...] = _rms(x_ref[...], g_ref[...]).astype(BF16)

    acc = jnp.dot(xn_ref[...], w_ref[...], preferred_element_type=F32)

    @pl.when(j == 0)
    def _():
        u_ref[...] = acc
        proj_ref[...] = acc.astype(BF16)

    @pl.when((j >= 1) & (j <= n_q_tiles + 1))
    def _():
        gain = jnp.where(j <= n_q_tiles, qg_ref[...] * q_scale, kg_ref[...])
        cos = cos_ref[...]
        sina = sina_ref[...]
        sinb = sinb_ref[...]
        for hh in range(acc.shape[1] // ATT_HEAD_DIM):
            sl = slice(hh * ATT_HEAD_DIM, (hh + 1) * ATT_HEAD_DIM)
            yn = _rms(acc[:, sl], gain)
            out = yn * cos + pltpu.roll(yn, 96, 1) * sina + pltpu.roll(yn, 32, 1) * sinb
            proj_ref[:, sl] = out.astype(BF16)

    @pl.when(j == n_q_tiles + 2)
    def _():
        proj_ref[...] = acc.astype(BF16)


def _rope_tables(seq_len):
    rows = seq_len // GRID_W
    row = np.repeat(np.arange(rows), GRID_W).astype(np.float64)
    col = np.tile(np.arange(GRID_W), rows).astype(np.float64)
    half = ATT_HEAD_DIM // 2
    inv = 1.0 / (ROPE_THETA ** (np.arange(0, half, 2, dtype=np.float64) / half))
    a_r = row[:, None] * inv[None, :]
    a_c = col[:, None] * inv[None, :]
    ang = np.concatenate([a_r, a_r, a_c, a_c], axis=-1)
    cos = np.cos(ang)
    sin = np.sin(ang)
    first = (np.arange(ATT_HEAD_DIM) % half) < (half // 2)
    sina = np.where(first[None, :], -sin, 0.0)
    sinb = np.where(first[None, :], 0.0, sin)
    return (jnp.asarray(cos, F32), jnp.asarray(sina, F32), jnp.asarray(sinb, F32))


def _in_proj_ab(x2, g, w, qg, kg, seq_len, *, tm=512, tn=512):
    m, d = x2.shape
    n = w.shape[1]
    pool_w = tn
    q_w = n - pool_w - 2 * (ATT_KV_HEADS * ATT_HEAD_DIM)
    n_q_tiles = q_w // tn
    assert n_q_tiles * tn == q_w and n // tn == n_q_tiles + 3
    cos, sina, sinb = _rope_tables(seq_len)
    tiles_per_seq = seq_len // tm
    tab_spec = pl.BlockSpec((tm, ATT_HEAD_DIM), lambda i, j: (i % tiles_per_seq, 0))
    vec = lambda width: pl.BlockSpec((1, width), lambda i, j: (0, 0))
    kern = functools.partial(_in_proj_ab_kernel, n_q_tiles=n_q_tiles,
                             q_scale=ATT_HEAD_DIM ** -0.5 * LOG2E)
    return pl.pallas_call(
        kern,
        grid=(m // tm, n // tn),
        in_specs=[
            pl.BlockSpec((tm, d), lambda i, j: (i, 0)),
            vec(d),
            pl.BlockSpec((d, tn), lambda i, j: (0, j)),
            vec(ATT_HEAD_DIM), vec(ATT_HEAD_DIM),
            tab_spec, tab_spec, tab_spec,
        ],
        out_specs=[
            pl.BlockSpec((tm, tn), lambda i, j: (i, j)),
            pl.BlockSpec((tm, tn), lambda i, j: (i, 0)),
        ],
        out_shape=[jax.ShapeDtypeStruct((m, n), BF16), jax.ShapeDtypeStruct((m, pool_w), F32)],
        scratch_shapes=[pltpu.VMEM((tm, d), BF16)],
        compiler_params=_params("parallel", "arbitrary"),
        name="in_proj_ab",
    )(x2, g, w, qg, kg, cos, sina, sinb)


_POOL_PAD = 16


def _pool_kernel(u_ref, w_ref, s_ref, o_ref, pad_ref, *, rows):
    seq = u_ref.shape[0]
    gw = w_ref.shape[1]
    zeros = jnp.zeros((_POOL_PAD, gw), F32)
    pad_ref[0:_POOL_PAD, :] = zeros
    pad_ref[_POOL_PAD + seq:, :] = zeros
    for gi, win in enumerate(POOL_WINDOWS):
        gsl = slice(gi * gw, (gi + 1) * gw)
        pad_ref[_POOL_PAD:_POOL_PAD + seq, :] = u_ref[:, gsl]
        half = win // 2
        for r0 in range(0, seq, rows):
            blk = pad_ref[r0:r0 + rows + 2 * _POOL_PAD, :]
            total = blk[_POOL_PAD - half:_POOL_PAD - half + rows]
            for off in range(-half + 1, half):
                total = total + blk[_POOL_PAD + off:_POOL_PAD + off + rows]
            t = r0 + lax.broadcasted_iota(jnp.int32, (rows, 1), 0)
            cnt = jnp.minimum(t + half, seq) - jnp.maximum(t - half, 0)
            pooled = total / cnt.astype(F32) - blk[_POOL_PAD:_POOL_PAD + rows]
            y = jnp.dot(pooled.astype(BF16), w_ref[gi], preferred_element_type=F32)
            o_ref[r0:r0 + rows, gsl] = (y * s_ref[:, gsl]).astype(BF16)


def _pool_mixer(u2, pool_w, pool_scale, batch, seq_len, *, rows=512):
    m, pw = u2.shape
    gw = pool_w.shape[1]
    return pl.pallas_call(
        functools.partial(_pool_kernel, rows=rows),
        grid=(batch,),
        in_specs=[
            pl.BlockSpec((seq_len, pw), lambda b: (b, 0)),
            pl.BlockSpec(pool_w.shape, lambda b: (0, 0, 0)),
            pl.BlockSpec((1, pw), lambda b: (0, 0)),
        ],
        out_specs=pl.BlockSpec((seq_len, pw), lambda b: (b, 0)),
        out_shape=jax.ShapeDtypeStruct((m, pw), BF16),
        scratch_shapes=[pltpu.VMEM((seq_len + 2 * _POOL_PAD, gw), F32)],
        compiler_params=_params("parallel"),
        name="pool_mixer",
    )(u2, pool_w, pool_scale)


def _attn_kernel(*refs, group, tk):
    q_refs = refs[:group]
    k_ref, v_ref, o_ref = refs[group:]
    tq = q_refs[0].shape[0]
    hd = k_ref.shape[1]
    q = jnp.concatenate([r[...] for r in q_refs], axis=0)
    rows = group * tq

    def body(c, carry):
        m, l, acc = carry
        start = pl.multiple_of(c * tk, tk)
        kc = k_ref[pl.ds(start, tk), :]
        vc = v_ref[pl.ds(start, tk), :]
        s = lax.dot_general(q, kc, _NT, preferred_element_type=F32)
        m_new = jnp.maximum(m, jnp.max(s, axis=-1, keepdims=True))
        alpha = jnp.exp2(m - m_new)
        p = jnp.exp2(s - m_new)
        l = alpha * l + jnp.sum(p, axis=-1, keepdims=True)
        acc = alpha * acc + jnp.dot(p.astype(BF16), vc, preferred_element_type=F32)
        return m_new, l, acc

    init = (jnp.full((rows, 1), -jnp.inf, F32), jnp.zeros((rows, 1), F32), jnp.zeros((rows, hd), F32))
    _, l, acc = lax.fori_loop(0, k_ref.shape[0] // tk, body, init)
    out = acc / l
    for g in range(group):
        o_ref[:, g * hd:(g + 1) * hd] = out[g * tq:(g + 1) * tq].astype(BF16)


def _attention(proj, batch, seq_len, pool_w, q_w, *, tq=256, tk=512):
    m = proj.shape[0]
    hd = ATT_HEAD_DIM
    heads = q_w // hd
    group = heads // ATT_KV_HEADS
    nq = seq_len // tq
    q_col0 = pool_w // hd
    k_col0 = (pool_w + q_w) // hd
    v_col0 = k_col0 + ATT_KV_HEADS

    def q_spec(g):
        return pl.BlockSpec((tq, hd), lambda b, kh, qi: (b * nq + qi, q_col0 + kh * group + g))

    return pl.pallas_call(
        functools.partial(_attn_kernel, group=group, tk=tk),
        grid=(batch, ATT_KV_HEADS, nq),
        in_specs=[q_spec(g) for g in range(group)] + [
            pl.BlockSpec((seq_len, hd), lambda b, kh, qi: (b, k_col0 + kh)),
            pl.BlockSpec((seq_len, hd), lambda b, kh, qi: (b, v_col0 + kh)),
        ],
        out_specs=pl.BlockSpec((tq, group * hd), lambda b, kh, qi: (b * nq + qi, kh)),
        out_shape=jax.ShapeDtypeStruct((m, q_w), BF16),
        compiler_params=_params("parallel", "parallel", "arbitrary"),
        name="gqa_attention",
    )(*([proj] * (group + 2)))


def _out_proj_kernel(*refs, n_in):
    a_refs = refs[:n_in]
    w_refs = refs[n_in:2 * n_in]
    x_ref, o_ref = refs[2 * n_in:]
    acc = x_ref[...]
    for a_ref, w_ref in zip(a_refs, w_refs):
        acc = acc + jnp.dot(a_ref[...], w_ref[...], preferred_element_type=F32)
    o_ref[...] = acc


def _out_proj(acts, weights, x2, *, tm=1024, tn=512):
    m, n = x2.shape
    n_in = len(acts)
    in_specs = [pl.BlockSpec((tm, a.shape[1]), lambda i, j: (i, 0)) for a in acts]
    in_specs += [pl.BlockSpec((w.shape[0], tn), lambda i, j: (0, j)) for w in weights]
    in_specs += [pl.BlockSpec((tm, tn), lambda i, j: (i, j))]
    return pl.pallas_call(
        functools.partial(_out_proj_kernel, n_in=n_in),
        grid=(m // tm, n // tn),
        in_specs=in_specs,
        out_specs=pl.BlockSpec((tm, tn), lambda i, j: (i, j)),
        out_shape=jax.ShapeDtypeStruct((m, n), F32),
        compiler_params=_params("parallel", "arbitrary"),
        name="out_proj",
    )(*acts, *weights, x2)


def _ffn_kernel(x_ref, xp_ref, xs_ref, g_ref, wg_ref, wv_ref, cg_ref, cv_ref, bg_ref, bv_ref, wd_ref,
                o_ref, xe_ref, *, tiles_per_seq):
    i = pl.program_id(0)
    j = pl.program_id(1)
    tm = x_ref.shape[0]
    halo = xp_ref.shape[0]

    @pl.when(j == 0)
    def _():
        g = g_ref[...]
        x = x_ref[...]
        si = i % tiles_per_seq
        keep_prev = jnp.where(si > 0, 1.0, 0.0)
        keep_next = jnp.where(si < tiles_per_seq - 1, 1.0, 0.0)
        xe_ref[0:halo, :] = (_rms(xp_ref[...], g) * keep_prev).astype(BF16)
        xe_ref[halo:halo + tm, :] = _rms(x, g).astype(BF16)
        xe_ref[halo + tm:, :] = (_rms(xs_ref[...], g) * keep_next).astype(BF16)
        o_ref[...] = x

    xe = xe_ref[...]

    def conv(w_ref, c_ref, b_ref):
        p = jnp.dot(xe, w_ref[...], preferred_element_type=F32)
        c = c_ref[...]
        return (c[0:1] * p[halo - 1:halo - 1 + tm] + c[1:2] * p[halo:halo + tm]
                + c[2:3] * p[halo + 1:halo + 1 + tm] + b_ref[...])

    gate = conv(wg_ref, cg_ref, bg_ref)
    val = conv(wv_ref, cv_ref, bv_ref)
    act = gate * jax.nn.sigmoid(gate) * val
    o_ref[...] += jnp.dot(act.astype(BF16), wd_ref[...], preferred_element_type=F32)


def _conv_ffn(x2, g, w_up, conv_w, conv_b, w_down, seq_len, *, tm=512, tf=512):
    m, d = x2.shape
    d_ff = w_down.shape[0]
    nf = d_ff // tf
    halo = BF16_ROWS
    tiles_per_seq = seq_len // tm
    hb = tm // halo
    last_hb = m // halo - 1
    return pl.pallas_call(
        functools.partial(_ffn_kernel, tiles_per_seq=tiles_per_seq),
        grid=(m // tm, nf),
        in_specs=[
            pl.BlockSpec((tm, d), lambda i, j: (i, 0)),
            pl.BlockSpec((halo, d), lambda i, j: (jnp.maximum(i * hb - 1, 0), 0)),
            pl.BlockSpec((halo, d), lambda i, j: (jnp.minimum((i + 1) * hb, last_hb), 0)),
            pl.BlockSpec((1, d), lambda i, j: (0, 0)),
            pl.BlockSpec((d, tf), lambda i, j: (0, j)),
            pl.BlockSpec((d, tf), lambda i, j: (0, j + nf)),
            pl.BlockSpec((3, tf), lambda i, j: (0, j)),
            pl.BlockSpec((3, tf), lambda i, j: (0, j + nf)),
            pl.BlockSpec((1, tf), lambda i, j: (0, j)),
            pl.BlockSpec((1, tf), lambda i, j: (0, j + nf)),
            pl.BlockSpec((tf, d), lambda i, j: (j, 0)),
        ],
        out_specs=pl.BlockSpec((tm, d), lambda i, j: (i, 0)),
        out_shape=jax.ShapeDtypeStruct((m, d), F32),
        scratch_shapes=[pltpu.VMEM((tm + 2 * halo, d), BF16)],
        compiler_params=_params("parallel", "arbitrary"),
        name="conv_ffn",
    )(x2, x2, x2, g, w_up, w_up, conv_w, conv_w, conv_b, conv_b, w_down)


def _in_proj_c_kernel(x_ref, g_ref, w_ref, wgt_ref, bg_ref, q_ref, kt_ref, v_ref, og_ref, gt_ref, xn_ref,
                      *, nt, q_scale):
    j = pl.program_id(1)

    @pl.when(j == 0)
    def _():
        xn = _rms(x_ref[...], g_ref[...]).astype(BF16)
        xn_ref[...] = xn
        gt_ref[...] = lax.dot_general(wgt_ref[...], xn, _NT, preferred_element_type=F32) + bg_ref[...]

    acc = jnp.dot(xn_ref[...], w_ref[...], preferred_element_type=F32)

    @pl.when(j < nt)
    def _():
        q_ref[...] = (acc * q_scale).astype(BF16)

    @pl.when((j >= nt) & (j < 2 * nt))
    def _():
        kt_ref[...] = acc.T.astype(BF16)

    @pl.when((j >= 2 * nt) & (j < 3 * nt))
    def _():
        v_ref[...] = acc.astype(BF16)

    @pl.when(j >= 3 * nt)
    def _():
        og_ref[...] = jax.nn.sigmoid(acc).astype(BF16)


def _in_proj_c(x2, g, w, wgt, bg, seq_len, *, tm=512, tn=512):
    m, d = x2.shape
    width = w.shape[1] // 4
    nt = width // tn
    n_gate = wgt.shape[0]
    tiles_per_seq = seq_len // tm
    batch = m // seq_len

    def sec(k):
        return lambda i, j: (i, jnp.clip(j - k * nt, 0, nt - 1))

    def kt_map(i, j):
        return ((i // tiles_per_seq) * nt + jnp.clip(j - nt, 0, nt - 1), i % tiles_per_seq)

    act = jax.ShapeDtypeStruct((m, width), BF16)
    return pl.pallas_call(
        functools.partial(_in_proj_c_kernel, nt=nt, q_scale=(width // MLSTM_HEADS) ** -0.5),
        grid=(m // tm, 4 * nt),
        in_specs=[
            pl.BlockSpec((tm, d), lambda i, j: (i, 0)),
            pl.BlockSpec((1, d), lambda i, j: (0, 0)),
            pl.BlockSpec((d, tn), lambda i, j: (0, j)),
            pl.BlockSpec((n_gate, d), lambda i, j: (0, 0)),
            pl.BlockSpec((n_gate, 1), lambda i, j: (0, 0)),
        ],
        out_specs=[
            pl.BlockSpec((tm, tn), sec(0)),
            pl.BlockSpec((tn, tm), kt_map),
            pl.BlockSpec((tm, tn), sec(2)),
            pl.BlockSpec((tm, tn), sec(3)),
            pl.BlockSpec((n_gate, tm), lambda i, j: (0, i)),
        ],
        out_shape=[act, jax.ShapeDtypeStruct((batch * width, seq_len), BF16), act, act,
                   jax.ShapeDtypeStruct((n_gate, m), F32)],
        scratch_shapes=[pltpu.VMEM((tm, d), BF16)],
        compiler_params=_params("parallel", "arbitrary"),
        name="in_proj_c",
    )(x2, g, w, wgt, bg)


def _split_dot(x, tri):
    hi = x.astype(BF16)
    r1 = x - hi.astype(F32)
    mid = r1.astype(BF16)
    lo = (r1 - mid.astype(F32)).astype(BF16)
    dot = lambda a: jnp.dot(a, tri, preferred_element_type=F32)
    return dot(hi) + dot(mid) + dot(lo)


def _mlstm_kernel(q_ref, kt_ref, v_ref, og_ref, gates_ref, hn_ref, y_ref,
                  acc_ref, lf_ref, r_ref, c_ref, n_ref, m_ref, *, chunk, rows):
    seq, d = q_ref.shape
    L = chunk
    nc = seq // L
    t_idx = lax.broadcasted_iota(jnp.int32, (L, L), 0)
    s_idx = lax.broadcasted_iota(jnp.int32, (L, L), 1)
    masks = (s_idx <= t_idx, s_idx >= t_idx)

    for dr in range(2):
        i_pre = gates_ref[2 * dr]
        f_pre = gates_ref[2 * dr + 1]
        lf = jnp.minimum(f_pre, 0.0) - jnp.log(1.0 + jnp.exp(-jnp.abs(f_pre)))
        tri = jnp.where(masks[1 - dr], 1.0, 0.0).astype(BF16)
        lf_ref[dr] = lf
        r_ref[dr] = i_pre - _split_dot(lf, tri)
        c_ref[dr] = jnp.zeros((d, d), F32)
        n_ref[dr] = jnp.zeros((d, LANES), F32)
        m_ref[dr] = jnp.zeros(m_ref.shape[1:], F32)

    def chunk_step(dr, c, accumulate):
        start = pl.multiple_of(c * L, L)
        q = q_ref[pl.ds(start, L), :]
        kt = kt_ref[:, pl.ds(start, L)]
        v = v_ref[pl.ds(start, L), :]
        lf = lf_ref[dr, pl.ds(c, 1), :]
        rr = r_ref[dr, pl.ds(c, 1), :]
        m_st = m_ref[dr][0:1, 0:1]
        mask = masks[dr]
        bcol = jnp.sum(jnp.where(mask, lf, 0.0), axis=1, keepdims=True)
        rmat = jnp.where(mask, rr, -jnp.inf)
        mcol = jnp.maximum(m_st, jnp.max(rmat, axis=1, keepdims=True))
        w = jnp.exp(rmat - mcol)
        a = jnp.exp(m_st - mcol)
        sc = jnp.dot(q, kt, preferred_element_type=F32) * w
        qn = jnp.dot(q, n_ref[dr].astype(BF16), preferred_element_type=F32)[:, 0:1]
        den = jnp.sum(sc, axis=1, keepdims=True) + a * qn
        num = (jnp.dot(sc.astype(BF16), v, preferred_element_type=F32)
               + a * jnp.dot(q, c_ref[dr].astype(BF16), preferred_element_type=F32))
        h = num / jnp.maximum(jnp.abs(den), jnp.exp(-(bcol + mcol)))
        if accumulate:
            acc_ref[pl.ds(start, L), :] += h
        else:
            acc_ref[pl.ds(start, L), :] = h
        last = L - 1 if dr == 0 else 0
        decay = a[last:last + 1]
        kw = kt.astype(F32) * w[last:last + 1, :]
        c_ref[dr] = decay * c_ref[dr] + jnp.dot(kw.astype(BF16), v, preferred_element_type=F32)
        n_ref[dr] = decay * n_ref[dr] + jnp.sum(kw, axis=1, keepdims=True)
        m_ref[dr] = jnp.broadcast_to(bcol[last:last + 1] + mcol[last:last + 1], m_ref.shape[1:])

    def make_body(accumulate):
        def body(i, carry):
            chunk_step(0, i, accumulate)
            chunk_step(1, nc - 1 - i, accumulate)
            return carry
        return body

    lax.fori_loop(0, nc // 2, make_body(False), 0)
    lax.fori_loop(nc // 2, nc, make_body(True), 0)

    hn = hn_ref[...]

    def finish(t, carry):
        start = pl.multiple_of(t * rows, rows)
        hs = acc_ref[pl.ds(start, rows), :]
        y = _rms(hs, hn) * og_ref[pl.ds(start, rows), :].astype(F32)
        y_ref[pl.ds(start, rows), :] = y.astype(BF16)
        return carry

    lax.fori_loop(0, seq // rows, finish, 0)


def _mlstm(q, kt, v, og, gates, h_norm, batch, seq_len, *, rows=512):
    m, width = q.shape
    d = width // MLSTM_HEADS
    L = MLSTM_CHUNK
    nc = seq_len // L
    assert nc % 2 == 0
    act_spec = pl.BlockSpec((seq_len, d), lambda b, h: (b, h))
    return pl.pallas_call(
        functools.partial(_mlstm_kernel, chunk=L, rows=rows),
        grid=(batch, MLSTM_HEADS),
        in_specs=[
            act_spec,
            pl.BlockSpec((d, seq_len), lambda b, h: (b * MLSTM_HEADS + h, 0)),
            act_spec,
            act_spec,
            pl.BlockSpec((None, None, 4, nc, L), lambda b, h: (b, h, 0, 0, 0)),
            pl.BlockSpec((1, d), lambda b, h: (0, h)),
        ],
        out_specs=act_spec,
        out_shape=jax.ShapeDtypeStruct((m, width), BF16),
        scratch_shapes=[
            pltpu.VMEM((seq_len, d), F32),
            pltpu.VMEM((2, nc, L), F32),
            pltpu.VMEM((2, nc, L), F32),
            pltpu.VMEM((2, d, d), F32),
            pltpu.VMEM((2, d, LANES), F32),
            pltpu.VMEM((2, 8, LANES), F32),
        ],
        compiler_params=_params("parallel", "arbitrary"),
        name="mlstm",
    )(q, kt, v, og, gates, h_norm)


def kernel(x, norm_mix, norm_ffn, w_in_ab, pool_w, pool_scale, q_norm, k_norm, w_out_ab, w_in_c, b_gate_c,
           h_norm_c, w_out_c, w_up, conv_w, conv_b, w_down):
    batch, seq_len, d = x.shape
    depth = norm_mix.shape[0]
    x2 = x.reshape(batch * seq_len, d)
    row = lambda v: v.reshape(1, -1)
    for layer in range(depth):
        if layer % 2 == 0:
            e = layer // 2
            pw = pool_w.shape[1] * pool_w.shape[2]
            q_w = w_out_ab.shape[1] - pw
            proj, u = _in_proj_ab(x2, row(norm_mix[layer]), w_in_ab[e].astype(BF16),
                                  row(q_norm[e]), row(k_norm[e]), seq_len)
            pool_out = _pool_mixer(u, pool_w[e].astype(BF16), row(pool_scale[e]), batch, seq_len)
            att_out = _attention(proj, batch, seq_len, pw, q_w)
            w_out = w_out_ab[e].astype(BF16)
            x2 = _out_proj([pool_out, att_out], [w_out[:pw], w_out[pw:]], x2)
        else:
            o = layer // 2
            width = w_out_c.shape[1]
            w_in = w_in_c[o]
            n_gate = w_in.shape[1] - 4 * width
            q, kt, v, og, gt = _in_proj_c(
                x2, row(norm_mix[layer]), w_in[:, :4 * width].astype(BF16),
                w_in[:, 4 * width:].T.astype(BF16), b_gate_c[o].reshape(n_gate, 1), seq_len)
            nc = seq_len // MLSTM_CHUNK
            gates = gt.reshape(4, MLSTM_HEADS, batch, nc, MLSTM_CHUNK).transpose(2, 1, 0, 3, 4)
            y = _mlstm(q, kt, v, og, gates, row(h_norm_c[o]), batch, seq_len)
            x2 = _out_proj([y], [w_out_c[o].astype(BF16)], x2)
        x2 = _conv_ffn(x2, row(norm_ffn[layer]), w_up[layer].astype(BF16), conv_w[layer],
                       row(conv_b[layer]), w_down[layer].astype(BF16), seq_len)
    return x2.reshape(batch, seq_len, d)
```

```python
import functools
import math

import numpy as np
import jax
import jax.numpy as jnp
from jax import lax
from jax.experimental import pallas as pl
from jax.experimental.pallas import tpu as pltpu

F32 = jnp.float32
BF16 = jnp.bfloat16

NORM_EPS = 1e-6
GRID_W = 64
ROPE_THETA = 10000.0
POOL_WINDOWS = (2, 4, 8, 16)
ATT_HEAD_DIM = 128
ATT_KV_HEADS = 4
MLSTM_HEADS = 8
MLSTM_CHUNK = 256
LANES = 128
BF16_ROWS = 16
LOG2E = math.log2(math.e)
VMEM_LIMIT = 56 * 1024 * 1024

_NT = (((1,), (1,)), ((), ()))


def _params(*sem):
    return pltpu.CompilerParams(dimension_semantics=sem, vmem_limit_bytes=VMEM_LIMIT)


def _rms(x, g):
    ms = jnp.mean(x * x, axis=-1, keepdims=True)
    return x * lax.rsqrt(ms + NORM_EPS) * g


def _layer_vec(width, layer):
    return pl.BlockSpec((None, 1, width), lambda i, j: (layer, 0, 0))


def _in_proj_ab_kernel(x_ref, g_ref, w_ref, qg_ref, kg_ref, cos_ref, sina_ref, sinb_ref,
                       qkv_ref, u_ref, xn_ref, *, n_q_tiles, q_scale):
    j = pl.program_id(1)

    @pl.when(j == 0)
    def _():
        xn_ref[...] = _rms(x_ref[...], g_ref[...]).astype(BF16)

    acc = jnp.dot(xn_ref[...], w_ref[...], preferred_element_type=F32)

    @pl.when(j == 0)
    def _():
        u_ref[...] = acc

    @pl.when((j >= 1) & (j <= n_q_tiles + 1))
    def _():
        gain = jnp.where(j <= n_q_tiles, qg_ref[...] * q_scale, kg_ref[...])
        cos = cos_ref[...]
        sina = sina_ref[...]
        sinb = sinb_ref[...]
        for hh in range(acc.shape[1] // ATT_HEAD_DIM):
            sl = slice(hh * ATT_HEAD_DIM, (hh + 1) * ATT_HEAD_DIM)
            yn = _rms(acc[:, sl], gain)
            out = yn * cos + pltpu.roll(yn, 96, 1) * sina + pltpu.roll(yn, 32, 1) * sinb
            qkv_ref[:, sl] = out.astype(BF16)

    @pl.when(j == n_q_tiles + 2)
    def _():
        qkv_ref[...] = acc.astype(BF16)

def _rope_tables(seq_len):
    rows = seq_len // GRID_W
    row = np.repeat(np.arange(rows), GRID_W).astype(np.float64)
    col = np.tile(np.arange(GRID_W), rows).astype(np.float64)
    half = ATT_HEAD_DIM // 2
    inv = 1.0 / (ROPE_THETA ** (np.arange(0, half, 2, dtype=np.float64) / half))
    a_r = row[:, None] * inv[None, :]
    a_c = col[:, None] * inv[None, :]
    ang = np.concatenate([a_r, a_r, a_c, a_c], axis=-1)
    cos = np.cos(ang)
    sin = np.sin(ang)
    first = (np.arange(ATT_HEAD_DIM) % half) < (half // 2)
    sina = np.where(first[None, :], -sin, 0.0)
    sinb = np.where(first[None, :], 0.0, sin)
    return (jnp.asarray(cos, F32), jnp.asarray(sina, F32), jnp.asarray(sinb, F32))


def _in_proj_ab(x2, g, w, qg, kg, layer, e, seq_len, pool_w, *, tm=512, tn=512):
    m, d = x2.shape
    n = w.shape[2]
    kv_w = ATT_KV_HEADS * ATT_HEAD_DIM
    q_w = n - pool_w - 2 * kv_w
    n_q_tiles = q_w // tn
    assert pool_w == tn and kv_w == tn and n_q_tiles * tn == q_w
    cos, sina, sinb = _rope_tables(seq_len)
    tiles_per_seq = seq_len // tm
    tab_spec = pl.BlockSpec((tm, ATT_HEAD_DIM), lambda i, j: (i % tiles_per_seq, 0))
    kern = functools.partial(_in_proj_ab_kernel, n_q_tiles=n_q_tiles,
                             q_scale=ATT_HEAD_DIM ** -0.5 * LOG2E)
    return pl.pallas_call(
        kern,
        grid=(m // tm, n // tn),
        in_specs=[
            pl.BlockSpec((tm, d), lambda i, j: (i, 0)),
            _layer_vec(d, layer),
            pl.BlockSpec((None, d, tn), lambda i, j: (e, 0, j)),
            _layer_vec(ATT_HEAD_DIM, e), _layer_vec(ATT_HEAD_DIM, e),
            tab_spec, tab_spec, tab_spec,
        ],
        out_specs=[
            pl.BlockSpec((tm, tn), lambda i, j: (i, jnp.maximum(j - 1, 0))),
            pl.BlockSpec((tm, tn), lambda i, j: (i, 0)),
        ],
        out_shape=[jax.ShapeDtypeStruct((m, q_w + 2 * kv_w), BF16),
                   jax.ShapeDtypeStruct((m, pool_w), F32)],
        scratch_shapes=[pltpu.VMEM((tm, d), BF16)],
        compiler_params=_params("parallel", "arbitrary"),
        name="in_proj_ab",
    )(x2, g, w, qg, kg, cos, sina, sinb)


_POOL_PAD = 16


def _pool_kernel(u_ref, w_ref, s_ref, o_ref, pad_ref, *, rows):
    seq = u_ref.shape[0]
    gw = w_ref.shape[1]
    zeros = jnp.zeros((_POOL_PAD, gw), F32)
    pad_ref[0:_POOL_PAD, :] = zeros
    pad_ref[_POOL_PAD + seq:, :] = zeros
    for gi, win in enumerate(POOL_WINDOWS):
        gsl = slice(gi * gw, (gi + 1) * gw)
        pad_ref[_POOL_PAD:_POOL_PAD + seq, :] = u_ref[:, gsl]
        half = win // 2
        for r0 in range(0, seq, rows):
            blk = pad_ref[r0:r0 + rows + 2 * _POOL_PAD, :]
            total = blk[_POOL_PAD - half:_POOL_PAD - half + rows]
            for off in range(-half + 1, half):
                total = total + blk[_POOL_PAD + off:_POOL_PAD + off + rows]
            t = r0 + lax.broadcasted_iota(jnp.int32, (rows, 1), 0)
            cnt = jnp.minimum(t + half, seq) - jnp.maximum(t - half, 0)
            pooled = total / cnt.astype(F32) - blk[_POOL_PAD:_POOL_PAD + rows]
            y = jnp.dot(pooled.astype(BF16), w_ref[gi], preferred_element_type=F32)
            o_ref[r0:r0 + rows, gsl] = (y * s_ref[:, gsl]).astype(BF16)


def _pool_mixer(u2, pool_w, pool_scale, e, batch, seq_len, *, rows=512):
    m, pw = u2.shape
    gw = pool_w.shape[2]
    return pl.pallas_call(
        functools.partial(_pool_kernel, rows=rows),
        grid=(batch,),
        in_specs=[
            pl.BlockSpec((seq_len, pw), lambda b: (b, 0)),
            pl.BlockSpec((None,) + pool_w.shape[1:], lambda b: (e, 0, 0, 0)),
            pl.BlockSpec((None, 1, pw), lambda b: (e, 0, 0)),
        ],
        out_specs=pl.BlockSpec((seq_len, pw), lambda b: (b, 0)),
        out_shape=jax.ShapeDtypeStruct((m, pw), BF16),
        scratch_shapes=[pltpu.VMEM((seq_len + 2 * _POOL_PAD, gw), F32)],
        compiler_params=_params("parallel"),
        name="pool_mixer",
    )(u2, pool_w, pool_scale)


def _attn_kernel(*refs, group, tk):
    q_refs = refs[:group]
    k_ref, v_ref, o_ref = refs[group:]
    tq = q_refs[0].shape[0]
    hd = k_ref.shape[1]
    q = jnp.concatenate([r[...] for r in q_refs], axis=0)
    rows = group * tq
    ones = jnp.ones((tk, hd), BF16)

    def body(c, carry):
        m, acc = carry
        start = pl.multiple_of(c * tk, tk)
        kc = k_ref[pl.ds(start, tk), :]
        v_aug = jnp.concatenate([v_ref[pl.ds(start, tk), :], ones], axis=1)
        s = lax.dot_general(q, kc, _NT, preferred_element_type=F32)
        m_new = jnp.maximum(m, jnp.max(s, axis=-1, keepdims=True))
        alpha = jnp.exp2(m - m_new)
        p = jnp.exp2(s - m_new)
        acc = alpha * acc + jnp.dot(p.astype(BF16), v_aug, preferred_element_type=F32)
        return m_new, acc

    init = (jnp.full((rows, 1), -jnp.inf, F32), jnp.zeros((rows, 2 * hd), F32))
    _, acc = lax.fori_loop(0, k_ref.shape[0] // tk, body, init, unroll=True)
    out = acc[:, :hd] / acc[:, hd:]
    for g in range(group):
        o_ref[:, g * hd:(g + 1) * hd] = out[g * tq:(g + 1) * tq].astype(BF16)


def _attention(qkv, batch, seq_len, q_w, *, tq=256, tk=512):
    m = qkv.shape[0]
    hd = ATT_HEAD_DIM
    group = q_w // hd // ATT_KV_HEADS
    nq = seq_len // tq
    k_col0 = q_w // hd
    v_col0 = k_col0 + ATT_KV_HEADS

    def q_spec(g):
        return pl.BlockSpec((tq, hd), lambda b, kh, qi: (b * nq + qi, kh * group + g))

    return pl.pallas_call(
        functools.partial(_attn_kernel, group=group, tk=tk),
        grid=(batch, ATT_KV_HEADS, nq),
        in_specs=[q_spec(g) for g in range(group)] + [
            pl.BlockSpec((seq_len, hd), lambda b, kh, qi: (b, k_col0 + kh)),
            pl.BlockSpec((seq_len, hd), lambda b, kh, qi: (b, v_col0 + kh)),
        ],
        out_specs=pl.BlockSpec((tq, group * hd), lambda b, kh, qi: (b * nq + qi, kh)),
        out_shape=jax.ShapeDtypeStruct((m, q_w), BF16),
        compiler_params=_params("parallel", "parallel", "arbitrary"),
        name="gqa_attention",
    )(*([qkv] * (group + 2)))


def _out_proj_kernel(*refs, n_in):
    a_refs = refs[:n_in]
    w_ref, x_ref, o_ref = refs[n_in:]
    acc = x_ref[...]
    row0 = 0
    for a_ref in a_refs:
        width = a_ref.shape[1]
        acc = acc + jnp.dot(a_ref[...], w_ref[row0:row0 + width, :], preferred_element_type=F32)
        row0 += width
    o_ref[...] = acc


def _out_proj(acts, w, layer_idx, x2, *, tm=1024, tn=512):
    m, n = x2.shape
    k = w.shape[1]
    assert sum(a.shape[1] for a in acts) == k
    in_specs = [pl.BlockSpec((tm, a.shape[1]), lambda i, j: (i, 0)) for a in acts]
    in_specs += [pl.BlockSpec((None, k, tn), lambda i, j: (layer_idx, 0, j)),
                 pl.BlockSpec((tm, tn), lambda i, j: (i, j))]
    return pl.pallas_call(
        functools.partial(_out_proj_kernel, n_in=len(acts)),
        grid=(m // tm, n // tn),
        in_specs=in_specs,
        out_specs=pl.BlockSpec((tm, tn), lambda i, j: (i, j)),
        out_shape=jax.ShapeDtypeStruct((m, n), F32),
        compiler_params=_params("parallel", "arbitrary"),
        name="out_proj",
    )(*acts, w, x2)


def _ffn_kernel(x_ref, xp_ref, xs_ref, g_ref, wg_ref, wv_ref, cg_ref, cv_ref, bg_ref, bv_ref, wd_ref,
                o_ref, xe_ref, *, tiles_per_seq):
    i = pl.program_id(0)
    j = pl.program_id(1)
    tm = x_ref.shape[0]
    halo = xp_ref.shape[0]

    @pl.when(j == 0)
    def _():
        g = g_ref[...]
        x = x_ref[...]
        si = i % tiles_per_seq
        keep_prev = jnp.where(si > 0, 1.0, 0.0)
        keep_next = jnp.where(si < tiles_per_seq - 1, 1.0, 0.0)
        xe_ref[0:halo, :] = (_rms(xp_ref[...], g) * keep_prev).astype(BF16)
        xe_ref[halo:halo + tm, :] = _rms(x, g).astype(BF16)
        xe_ref[halo + tm:, :] = (_rms(xs_ref[...], g) * keep_next).astype(BF16)
        o_ref[...] = x

    xe = xe_ref[...]

    def conv(w_ref, c_ref, b_ref):
        p = jnp.dot(xe, w_ref[...], preferred_element_type=F32)
        c = c_ref[...]
        return (c[0:1] * p[halo - 1:halo - 1 + tm] + c[1:2] * p[halo:halo + tm]
                + c[2:3] * p[halo + 1:halo + 1 + tm] + b_ref[...])

    gate = conv(wg_ref, cg_ref, bg_ref)
    val = conv(wv_ref, cv_ref, bv_ref)
    act = gate * jax.nn.sigmoid(gate) * val
    o_ref[...] += jnp.dot(act.astype(BF16), wd_ref[...], preferred_element_type=F32)


def _conv_ffn(x2, g, w_up, conv_w, conv_b, w_down, layer, seq_len, *, tm=512, tf=512):
    m, d = x2.shape
    d_ff = w_down.shape[1]
    nf = d_ff // tf
    halo = BF16_ROWS
    tiles_per_seq = seq_len // tm
    hb = tm // halo
    last_hb = m // halo - 1
    return pl.pallas_call(
        functools.partial(_ffn_kernel, tiles_per_seq=tiles_per_seq),
        grid=(m // tm, nf),
        in_specs=[
            pl.BlockSpec((tm, d), lambda i, j: (i, 0)),
            pl.BlockSpec((halo, d), lambda i, j: (jnp.maximum(i * hb - 1, 0), 0)),
            pl.BlockSpec((halo, d), lambda i, j: (jnp.minimum((i + 1) * hb, last_hb), 0)),
            _layer_vec(d, layer),
            pl.BlockSpec((None, d, tf), lambda i, j: (layer, 0, j)),
            pl.BlockSpec((None, d, tf), lambda i, j: (layer, 0, j + nf)),
            pl.BlockSpec((None, 3, tf), lambda i, j: (layer, 0, j)),
            pl.BlockSpec((None, 3, tf), lambda i, j: (layer, 0, j + nf)),
            pl.BlockSpec((None, 1, tf), lambda i, j: (layer, 0, j)),
            pl.BlockSpec((None, 1, tf), lambda i, j: (layer, 0, j + nf)),
            pl.BlockSpec((None, tf, d), lambda i, j: (layer, j, 0)),
        ],
        out_specs=pl.BlockSpec((tm, d), lambda i, j: (i, 0)),
        out_shape=jax.ShapeDtypeStruct((m, d), F32),
        scratch_shapes=[pltpu.VMEM((tm + 2 * halo, d), BF16)],
        compiler_params=_params("parallel", "arbitrary"),
        name="conv_ffn",
    )(x2, x2, x2, g, w_up, w_up, conv_w, conv_w, conv_b, conv_b, w_down)


def _in_proj_c_kernel(x_ref, g_ref, w_ref, wgt_ref, bg_ref, q_ref, kt_ref, v_ref, og_ref, gt_ref, xn_ref,
                      *, nt, q_scale):
    j = pl.program_id(1)

    @pl.when(j == 0)
    def _():
        xn = _rms(x_ref[...], g_ref[...]).astype(BF16)
        xn_ref[...] = xn
        gt_ref[...] = lax.dot_general(wgt_ref[...], xn, _NT, preferred_element_type=F32) + bg_ref[...]

    acc = jnp.dot(xn_ref[...], w_ref[...], preferred_element_type=F32)

    @pl.when(j < nt)
    def _():
        q_ref[...] = (acc * q_scale).astype(BF16)

    @pl.when((j >= nt) & (j < 2 * nt))
    def _():
        kt_ref[...] = acc.T.astype(BF16)

    @pl.when((j >= 2 * nt) & (j < 3 * nt))
    def _():
        v_ref[...] = acc.astype(BF16)

    @pl.when(j >= 3 * nt)
    def _():
        og_ref[...] = jax.nn.sigmoid(acc).astype(BF16)


def _in_proj_c(x2, g, w, wgt, bg, layer, o, seq_len, width, *, tm=512, tn=512):
    m, d = x2.shape
    nt = width // tn
    n_gate = wgt.shape[0]
    tiles_per_seq = seq_len // tm
    batch = m // seq_len

    def sec(k):
        return lambda i, j: (i, jnp.clip(j - k * nt, 0, nt - 1))

    def kt_map(i, j):
        return ((i // tiles_per_seq) * nt + jnp.clip(j - nt, 0, nt - 1), i % tiles_per_seq)

    act = jax.ShapeDtypeStruct((m, width), BF16)
    return pl.pallas_call(
        functools.partial(_in_proj_c_kernel, nt=nt, q_scale=(width // MLSTM_HEADS) ** -0.5),
        grid=(m // tm, 4 * nt),
        in_specs=[
            pl.BlockSpec((tm, d), lambda i, j: (i, 0)),
            _layer_vec(d, layer),
            pl.BlockSpec((None, d, tn), lambda i, j: (o, 0, j)),
            pl.BlockSpec((n_gate, d), lambda i, j: (0, 0)),
            pl.BlockSpec((n_gate, 1), lambda i, j: (0, 0)),
        ],
        out_specs=[
            pl.BlockSpec((tm, tn), sec(0)),
            pl.BlockSpec((tn, tm), kt_map),
            pl.BlockSpec((tm, tn), sec(2)),
            pl.BlockSpec((tm, tn), sec(3)),
            pl.BlockSpec((n_gate, tm), lambda i, j: (0, i)),
        ],
        out_shape=[act, jax.ShapeDtypeStruct((batch * width, seq_len), BF16), act, act,
                   jax.ShapeDtypeStruct((n_gate, m), F32)],
        scratch_shapes=[pltpu.VMEM((tm, d), BF16)],
        compiler_params=_params("parallel", "arbitrary"),
        name="in_proj_c",
    )(x2, g, w, wgt, bg)


def _split_dot(x, tri):
    hi = x.astype(BF16)
    r1 = x - hi.astype(F32)
    mid = r1.astype(BF16)
    lo = (r1 - mid.astype(F32)).astype(BF16)
    dot = lambda a: jnp.dot(a, tri, preferred_element_type=F32)
    return dot(hi) + dot(mid) + dot(lo)


def _mlstm_kernel(q_ref, kt_ref, v_ref, og_ref, gates_ref, hn_ref, y_ref,
                  acc_ref, lf_ref, r_ref, c_ref, n_ref, m_ref, *, chunk, rows):
    seq, d = q_ref.shape
    L = chunk
    nc = seq // L
    t_idx = lax.broadcasted_iota(jnp.int32, (L, L), 0)
    s_idx = lax.broadcasted_iota(jnp.int32, (L, L), 1)
    masks = (s_idx <= t_idx, s_idx >= t_idx)

    for dr in range(2):
        i_pre = gates_ref[2 * dr]
        f_pre = gates_ref[2 * dr + 1]
        lf = jnp.minimum(f_pre, 0.0) - jnp.log(1.0 + jnp.exp(-jnp.abs(f_pre)))
        tri = jnp.where(masks[1 - dr], 1.0, 0.0).astype(BF16)
        lf_ref[dr] = lf
        r_ref[dr] = i_pre - _split_dot(lf, tri)
        c_ref[dr] = jnp.zeros((d, d), F32)
        n_ref[dr] = jnp.zeros((d, LANES), F32)
        m_ref[dr] = jnp.zeros(m_ref.shape[1:], F32)

    def chunk_step(dr, c, accumulate):
        start = pl.multiple_of(c * L, L)
        q = q_ref[pl.ds(start, L), :]
        kt = kt_ref[:, pl.ds(start, L)]
        v = v_ref[pl.ds(start, L), :]
        lf = lf_ref[dr, pl.ds(c, 1), :]
        rr = r_ref[dr, pl.ds(c, 1), :]
        m_st = m_ref[dr][0:1, 0:1]
        mask = masks[dr]
        bcol = jnp.sum(jnp.where(mask, lf, 0.0), axis=1, keepdims=True)
        rmat = jnp.where(mask, rr, -jnp.inf)
        mcol = jnp.maximum(m_st, jnp.max(rmat, axis=1, keepdims=True))
        w = jnp.exp(rmat - mcol)
        a = jnp.exp(m_st - mcol)
        sc = jnp.dot(q, kt, preferred_element_type=F32) * w
        qn = jnp.dot(q, n_ref[dr].astype(BF16), preferred_element_type=F32)[:, 0:1]
        den = jnp.sum(sc, axis=1, keepdims=True) + a * qn
        num = (jnp.dot(sc.astype(BF16), v, preferred_element_type=F32)
               + a * jnp.dot(q, c_ref[dr].astype(BF16), preferred_element_type=F32))
        h = num / jnp.maximum(jnp.abs(den), jnp.exp(-(bcol + mcol)))
        if accumulate:
            acc_ref[pl.ds(start, L), :] += h
        else:
            acc_ref[pl.ds(start, L), :] = h
        last = L - 1 if dr == 0 else 0
        decay = a[last:last + 1]
        kw = kt.astype(F32) * w[last:last + 1, :]
        c_ref[dr] = decay * c_ref[dr] + jnp.dot(kw.astype(BF16), v, preferred_element_type=F32)
        n_ref[dr] = decay * n_ref[dr] + jnp.sum(kw, axis=1, keepdims=True)
        m_ref[dr] = jnp.broadcast_to(bcol[last:last + 1] + mcol[last:last + 1], m_ref.shape[1:])

    def make_body(accumulate):
        def body(i, carry):
            chunk_step(0, i, accumulate)
            chunk_step(1, nc - 1 - i, accumulate)
            return carry
        return body

    lax.fori_loop(0, nc // 2, make_body(False), 0)
    lax.fori_loop(nc // 2, nc, make_body(True), 0)

    hn = hn_ref[...]

    def finish(t, carry):
        start = pl.multiple_of(t * rows, rows)
        hs = acc_ref[pl.ds(start, rows), :]
        y = _rms(hs, hn) * og_ref[pl.ds(start, rows), :].astype(F32)
        y_ref[pl.ds(start, rows), :] = y.astype(BF16)
        return carry

    lax.fori_loop(0, seq // rows, finish, 0)


def _mlstm(q, kt, v, og, gates, h_norm, o, batch, seq_len, *, rows=512):
    m, width = q.shape
    d = width // MLSTM_HEADS
    L = MLSTM_CHUNK
    nc = seq_len // L
    assert nc % 2 == 0
    act_spec = pl.BlockSpec((seq_len, d), lambda b, h: (b, h))
    return pl.pallas_call(
        functools.partial(_mlstm_kernel, chunk=L, rows=rows),
        grid=(batch, MLSTM_HEADS),
        in_specs=[
            act_spec,
            pl.BlockSpec((d, seq_len), lambda b, h: (b * MLSTM_HEADS + h, 0)),
            act_spec,
            act_spec,
            pl.BlockSpec((None, None, 4, nc, L), lambda b, h: (b, h, 0, 0, 0)),
            pl.BlockSpec((None, 1, d), lambda b, h: (o, 0, h)),
        ],
        out_specs=act_spec,
        out_shape=jax.ShapeDtypeStruct((m, width), BF16),
        scratch_shapes=[
            pltpu.VMEM((seq_len, d), F32),
            pltpu.VMEM((2, nc, L), F32),
            pltpu.VMEM((2, nc, L), F32),
            pltpu.VMEM((2, d, d), F32),
            pltpu.VMEM((2, d, LANES), F32),
            pltpu.VMEM((2, 8, LANES), F32),
        ],
        compiler_params=_params("parallel", "arbitrary"),
        name="mlstm",
    )(q, kt, v, og, gates, h_norm)


def kernel(x, norm_mix, norm_ffn, w_in_ab, pool_w, pool_scale, q_norm, k_norm, w_out_ab, w_in_c, b_gate_c,
           h_norm_c, w_out_c, w_up, conv_w, conv_b, w_down):
    batch, seq_len, d = x.shape
    depth = norm_mix.shape[0]
    x2 = x.reshape(batch * seq_len, d)
    rows3 = lambda p: p.reshape(p.shape[0], 1, p.shape[1])
    norm_mix, norm_ffn, pool_scale, q_norm, k_norm, h_norm_c, conv_b = map(
        rows3, (norm_mix, norm_ffn, pool_scale, q_norm, k_norm, h_norm_c, conv_b))
    w_in_ab, pool_w, w_out_ab, w_out_c, w_up, w_down = (
        p.astype(BF16) for p in (w_in_ab, pool_w, w_out_ab, w_out_c, w_up, w_down))
    width_c = w_out_c.shape[1]
    w_in_c_main = w_in_c.astype(BF16)
    for layer in range(depth):
        if layer % 2 == 0:
            e = layer // 2
            pw = pool_w.shape[1] * pool_w.shape[2]
            q_w = w_out_ab.shape[1] - pw
            qkv, u = _in_proj_ab(x2, norm_mix, w_in_ab, q_norm, k_norm, layer, e, seq_len, pw)
            pool_out = _pool_mixer(u, pool_w, pool_scale, e, batch, seq_len)
            att_out = _attention(qkv, batch, seq_len, q_w)
            x2 = _out_proj([pool_out, att_out], w_out_ab, e, x2)
        else:
            o = layer // 2
            n_gate = w_in_c.shape[2] - 4 * width_c
            wgt = w_in_c[o, :, 4 * width_c:].T.astype(BF16)
            q, kt, v, og, gt = _in_proj_c(x2, norm_mix, w_in_c_main, wgt, b_gate_c[o].reshape(n_gate, 1),
                                          layer, o, seq_len, width_c)
            nc = seq_len // MLSTM_CHUNK
            gates = gt.reshape(4, MLSTM_HEADS, batch, nc, MLSTM_CHUNK).transpose(2, 1, 0, 3, 4)
            y = _mlstm(q, kt, v, og, gates, h_norm_c, o, batch, seq_len)
            x2 = _out_proj([y], w_out_c, o, x2)
        x2 = _conv_ffn(x2, norm_ffn, w_up, conv_w, conv_b, w_down, layer, seq_len)
    return x2.reshape(batch, seq_len, d)
```
